```python
import math
import jax, jax.numpy as jnp
from jax import lax
import numpy as np

D_MODEL = 1024
BATCH = 8
SEQ = 2048
DEPTH = 4
DEC_BATCH = 128
DEC_SEQ = 1
PAST_LEN = 2048
PAGE_SIZE = 128

HEAD_DIM = 64
NSA_HEADS = 8
NSA_KV_HEADS = 2
NSA_GROUP = NSA_HEADS // NSA_KV_HEADS
SB_HEADS = 8
NSA_WIDTH = NSA_HEADS * HEAD_DIM
KV_WIDTH = NSA_KV_HEADS * HEAD_DIM
SB_WIDTH = SB_HEADS * HEAD_DIM
N_BRANCH = 3
MIX_WIDTH = NSA_WIDTH + SB_WIDTH
EVEN_SPLIT = (NSA_WIDTH, KV_WIDTH, KV_WIDTH, KV_WIDTH, KV_WIDTH, KV_WIDTH, KV_WIDTH, N_BRANCH * NSA_HEADS, SB_WIDTH, SB_WIDTH, SB_WIDTH)
EVEN_IN = sum(EVEN_SPLIT)
CMP_BLOCK = 32
CMP_STRIDE = 16
CMP_HIDDEN = 2 * HEAD_DIM
SEL_BLOCK = 64
N_SEL = 16
WINDOW = 512
QBLOCK = 128
SEL_QBLOCK = 32
N_BUCKETS = 32
MAX_DISTANCE = 128
CHUNK = 128
C_GROUPS = 8
C_HALF = D_MODEL
C_GROUP_DIM = C_HALF // C_GROUPS
N_KEYS = 128
N_EXPERTS = N_KEYS * N_KEYS
PEER_HEADS = 8
PEER_TOPK = 16
D_KEY = 256
PEER_TBLOCK = 256
N_ATTN_LAYERS = (DEPTH + 1) // 2
N_CHUNK_LAYERS = DEPTH // 2
EPS = 1e-6
NEG = -1e30
FORCE_SCORE = 1e4
SCALE = HEAD_DIM ** -0.5

kernel_name = 'nsa_stickbreak_gmlp_peer_decode_step'


def rms_norm(x, g):
    xf = x.astype(jnp.float32)
    y = xf * lax.rsqrt(jnp.mean(xf * xf, axis=-1, keepdims=True) + EPS)
    return (y * g.astype(jnp.float32)).astype(x.dtype)


def layer_norm(x, g, b):
    xf = x.astype(jnp.float32)
    xc = xf - jnp.mean(xf, axis=-1, keepdims=True)
    y = xc * lax.rsqrt(jnp.mean(xc * xc, axis=-1, keepdims=True) + EPS)
    return (y * g.astype(jnp.float32) + b.astype(jnp.float32)).astype(x.dtype)


def masked_softmax(s, mask):
    p = jax.nn.softmax(jnp.where(mask, s, NEG), axis=-1)
    return jnp.where(mask, p, 0.0)


def t5_bucket(dist):
    n = jnp.maximum(dist, 0)
    exact = N_BUCKETS // 2
    nf = jnp.maximum(n, exact).astype(jnp.float32)
    large = exact + (jnp.log(nf / exact) / math.log(MAX_DISTANCE / exact) * (N_BUCKETS - exact)).astype(jnp.int32)
    return jnp.where(n < exact, n, jnp.minimum(large, N_BUCKETS - 1))


def head_bias(rel_bias, dist):
    b = rel_bias[t5_bucket(dist)].astype(jnp.float32)
    return jnp.transpose(b.reshape(dist.shape + (NSA_KV_HEADS, NSA_GROUP)), (2, 3, 0, 1))


def merge_blocks(o):
    o = jnp.moveaxis(o, 0, 1)
    return o.reshape((o.shape[0], o.shape[1] * o.shape[2]) + o.shape[3:])


def gather_pages(pool, page_table):
    g = pool[page_table]
    return g.reshape((page_table.shape[0], page_table.shape[1] * pool.shape[1]) + pool.shape[2:])


def compress(k_raw, pe, w1, w2):
    b, l = k_raw.shape[:2]
    n_cmp = (l - CMP_BLOCK) // CMP_STRIDE + 1
    idx = np.arange(n_cmp)[:, None] * CMP_STRIDE + np.arange(CMP_BLOCK)[None, :]
    blk = k_raw[:, idx] + pe[:, None, :]
    flat = jnp.moveaxis(blk, 3, 2).reshape(b, n_cmp, NSA_KV_HEADS, CMP_BLOCK * HEAD_DIM)
    return jax.nn.gelu(flat @ w1) @ w2


def cmp_branch(q, kc, vc, tq, rel_bias):
    b, t = q.shape[:2]
    n_cmp = kc.shape[1]
    qg = q.reshape(b, t, NSA_KV_HEADS, NSA_GROUP, HEAD_DIM)
    s = jnp.einsum('btgrd,bcgd->bgrtc', qg, kc).astype(jnp.float32) * SCALE
    ends = jnp.arange(n_cmp) * CMP_STRIDE + (CMP_BLOCK - 1)
    dist = tq[:, None] - ends[None, :]
    p = masked_softmax(s + head_bias(rel_bias, dist), dist >= 0)
    o = jnp.einsum('bgrtc,bcgd->btgrd', p.astype(vc.dtype), vc)
    return o.reshape(b, t, NSA_HEADS, HEAD_DIM), p


def select_blocks(p_cmp, tq, length):
    n_cmp = p_cmp.shape[-1]
    n_sb = -(-length // SEL_BLOCK)
    c0 = np.arange(n_cmp) * CMP_STRIDE
    j0 = np.arange(n_sb) * SEL_BLOCK
    overlap = ((c0[:, None] < j0[None, :] + SEL_BLOCK) & (c0[:, None] + CMP_BLOCK > j0[None, :])).astype(np.float32)
    score = jnp.einsum('bgrtc,cj->btgj', p_cmp, jnp.asarray(overlap))
    j = jnp.arange(n_sb)[None, :]
    cur = (tq // SEL_BLOCK)[:, None]
    forced = (j == 0) | (j == cur) | (j == cur - 1)
    valid = j * SEL_BLOCK <= tq[:, None]
    score = jnp.where(forced[None, :, None, :], FORCE_SCORE, score)
    score = jnp.where(valid[None, :, None, :], score, NEG)
    _, idx = lax.top_k(score, min(N_SEL, n_sb))
    return idx


def sel_branch(q, kv_blk, tq, idx, rel_bias):
    b, tb = q.shape[:2]
    k_sel = idx.shape[-1]
    kvg = jnp.moveaxis(kv_blk, 4, 1)
    ig = jnp.moveaxis(idx, 2, 1).reshape(b, NSA_KV_HEADS, tb * k_sel)
    gathered = jax.vmap(jax.vmap(lambda a, i: a[i]))(kvg, ig)
    gathered = gathered.reshape(b, NSA_KV_HEADS, tb, k_sel * SEL_BLOCK, 2, HEAD_DIM)
    qg = q.reshape(b, tb, NSA_KV_HEADS, NSA_GROUP, HEAD_DIM)
    s = jnp.einsum('btgrd,bgtnd->btgrn', qg, gathered[..., 0, :]).astype(jnp.float32) * SCALE
    kpos = (idx[..., None] * SEL_BLOCK + jnp.arange(SEL_BLOCK)).reshape(b, tb, NSA_KV_HEADS, k_sel * SEL_BLOCK)
    dist = tq[None, :, None, None] - kpos
    rel3 = rel_bias.reshape(N_BUCKETS, NSA_KV_HEADS, NSA_GROUP)
    bias = jnp.moveaxis(rel3[t5_bucket(dist), jnp.arange(NSA_KV_HEADS)[:, None]], -1, 3).astype(jnp.float32)
    p = masked_softmax(s + bias, (dist >= 0)[:, :, :, None, :])
    o = jnp.einsum('btgrn,bgtnd->btgrd', p.astype(gathered.dtype), gathered[..., 1, :])
    return o.reshape(b, tb, NSA_HEADS, HEAD_DIM)


def window_branch(q, kv, tq, tk, rel_bias):
    b, t = q.shape[:2]
    qg = q.reshape(b, t, NSA_KV_HEADS, NSA_GROUP, HEAD_DIM)
    s = jnp.einsum('btgrd,bsgd->bgrts', qg, kv[:, :, 0]).astype(jnp.float32) * SCALE
    dist = tq[:, None] - tk[None, :]
    mask = (dist >= 0) & (dist < WINDOW) & (tk[None, :] >= 0)
    p = masked_softmax(s + head_bias(rel_bias, dist), mask)
    o = jnp.einsum('bgrts,bsgd->btgrd', p.astype(kv.dtype), kv[:, :, 1])
    return o.reshape(b, t, NSA_HEADS, HEAD_DIM)


def stick_breaking(q, kv, tq, tk):
    z = jnp.einsum('bthd,bshd->bhts', q, kv[:, :, 0]).astype(jnp.float32) * SCALE
    mask = tk[None, :] < tq[:, None]
    log_1m = jnp.where(mask, jax.nn.log_sigmoid(-z), 0.0)
    later = lax.cumsum(log_1m, axis=3, reverse=True) - log_1m
    a = jnp.where(mask, jnp.exp(jax.nn.log_sigmoid(z) + later), 0.0)
    return jnp.einsum('bhts,bshd->bthd', a.astype(kv.dtype), kv[:, :, 1])


def attn_mixer(h, w_in, w_out, cmp_pe, cmp_w1, cmp_w2, rel_bias, past, offset):
    b, t, _ = h.shape
    offs = np.cumsum(EVEN_SPLIT)[:-1].tolist()
    q, ck, cv, sk, sv, wk, wv, gl, bq, bk, bv = jnp.split(h @ w_in, offs, axis=-1)
    hd = lambda a: a.reshape(b, t, -1, HEAD_DIM)
    q, bq = hd(q), hd(bq)
    new_cmp = jnp.stack([hd(ck), hd(cv)], axis=2)
    new_sel = jnp.stack([hd(sk), hd(sv)], axis=2)
    new_win = jnp.stack([hd(wk), hd(wv)], axis=2)
    new_sb = jnp.stack([hd(bk), hd(bv)], axis=2)
    gates = jax.nn.sigmoid(gl.astype(jnp.float32)).reshape(b, t, NSA_HEADS, N_BRANCH).astype(h.dtype)
    if past is None:
        cmp_all, sel_all, win_all, sb_all = new_cmp, new_sel, new_win, new_sb
    else:
        cat = lambda a, c: jnp.concatenate([a, c], axis=1)
        cmp_all, sel_all = cat(past[0], new_cmp), cat(past[1], new_sel)
        win_all, sb_all = cat(past[2], new_win), cat(past[3], new_sb)
    length = offset + t
    win_start = length - win_all.shape[1]
    tq = offset + jnp.arange(t)
    kc = compress(cmp_all[:, :, 0], cmp_pe[0], cmp_w1[0], cmp_w2[0])
    vc = compress(cmp_all[:, :, 1], cmp_pe[1], cmp_w1[1], cmp_w2[1])
    o_cmp, p_cmp = cmp_branch(q, kc, vc, tq, rel_bias)
    sel_idx = select_blocks(p_cmp, tq, length)
    n_sb = -(-length // SEL_BLOCK)
    sel_blk = jnp.pad(sel_all, ((0, 0), (0, n_sb * SEL_BLOCK - length), (0, 0), (0, 0), (0, 0)))
    sel_blk = sel_blk.reshape((b, n_sb, SEL_BLOCK) + sel_all.shape[2:])
    tk_all = jnp.arange(length)
    if past is None:
        def sel_fn(i):
            s0 = i * SEL_QBLOCK
            return sel_branch(lax.dynamic_slice_in_dim(q, s0, SEL_QBLOCK, 1), sel_blk, s0 + jnp.arange(SEL_QBLOCK),
                              lax.dynamic_slice_in_dim(sel_idx, s0, SEL_QBLOCK, 1), rel_bias)
        o_sel = merge_blocks(lax.map(sel_fn, jnp.arange(t // SEL_QBLOCK)))
        win_pad = jnp.pad(win_all, ((0, 0), (WINDOW, 0), (0, 0), (0, 0), (0, 0)))
        def win_fn(i):
            s0 = i * QBLOCK
            return window_branch(lax.dynamic_slice_in_dim(q, s0, QBLOCK, 1),
                                 lax.dynamic_slice_in_dim(win_pad, s0, WINDOW + QBLOCK, 1),
                                 s0 + jnp.arange(QBLOCK), s0 - WINDOW + jnp.arange(WINDOW + QBLOCK), rel_bias)
        o_win = merge_blocks(lax.map(win_fn, jnp.arange(t // QBLOCK)))
        def sb_fn(i):
            s0 = i * QBLOCK
            return stick_breaking(lax.dynamic_slice_in_dim(bq, s0, QBLOCK, 1), sb_all, s0 + jnp.arange(QBLOCK), tk_all)
        o_sb = merge_blocks(lax.map(sb_fn, jnp.arange(t // QBLOCK)))
    else:
        o_sel = sel_branch(q, sel_blk, tq, sel_idx, rel_bias)
        o_win = window_branch(q, win_all, tq, win_start + jnp.arange(win_all.shape[1]), rel_bias)
        o_sb = stick_breaking(bq, sb_all, tq, tk_all)
    o_nsa = gates[..., 0:1] * o_cmp + gates[..., 1:2] * o_sel + gates[..., 2:3] * o_win
    o = jnp.concatenate([o_nsa.reshape(b, t, NSA_WIDTH), o_sb.reshape(b, t, SB_WIDTH)], axis=-1)
    keep = min(WINDOW, length)
    win_state = win_all[:, win_all.shape[1] - keep:]
    return o @ w_out, (new_cmp, new_sel, win_state, new_sb)


def chunk_mixer(h, w_in, w_s, b_s, ln_g, ln_b, w_out):
    b, t, _ = h.shape
    u, v = jnp.split(jax.nn.gelu(h @ w_in), 2, axis=-1)
    v = layer_norm(v, ln_g, ln_b)
    n_chunks = -(-t // CHUNK)
    vp = jnp.pad(v, ((0, 0), (0, n_chunks * CHUNK - t), (0, 0))).reshape(b, n_chunks, CHUNK, C_GROUPS, C_GROUP_DIM)
    w_causal = w_s * jnp.tril(jnp.ones((CHUNK, CHUNK), w_s.dtype))
    mixed = jnp.einsum('gts,bcsgd->bctgd', w_causal, vp) + jnp.transpose(b_s)[:, :, None]
    mixed = mixed.reshape(b, n_chunks * CHUNK, C_HALF)[:, :t]
    return (u * mixed) @ w_out, v


def peer(h, w_q, sub_keys, expert_u, expert_v):
    b, t, d = h.shape
    n = b * t
    n_blk = -(-n // PEER_TBLOCK)
    xp = jnp.pad(h.reshape(n, d), ((0, n_blk * PEER_TBLOCK - n), (0, 0))).reshape(n_blk, PEER_TBLOCK, d)
    def block(xb):
        q = (xb @ w_q).reshape(PEER_TBLOCK, PEER_HEADS, 2, D_KEY // 2)
        s = jnp.einsum('thcd,hckd->thck', q, sub_keys).astype(jnp.float32)
        s1, i1 = lax.top_k(s[:, :, 0], PEER_TOPK)
        s2, i2 = lax.top_k(s[:, :, 1], PEER_TOPK)
        cand = (s1[..., :, None] + s2[..., None, :]).reshape(PEER_TBLOCK, PEER_HEADS, PEER_TOPK * PEER_TOPK)
        cand_id = (i1[..., :, None] * N_KEYS + i2[..., None, :]).reshape(PEER_TBLOCK, PEER_HEADS, PEER_TOPK * PEER_TOPK)
        top_s, top_j = lax.top_k(cand, PEER_TOPK)
        eid = jnp.take_along_axis(cand_id, top_j, axis=-1)
        g = jax.nn.softmax(top_s, axis=-1)
        act = jax.nn.gelu(jnp.einsum('thkd,td->thk', expert_u[eid], xb).astype(jnp.float32))
        return jnp.einsum('thk,thkd->td', (g * act).astype(xb.dtype), expert_v[eid])
    return lax.map(block, xp).reshape(n_blk * PEER_TBLOCK, d)[:n].reshape(b, t, d)


def setup_inputs(seed: int = 0) -> dict:
    key = jax.random.key(seed)
    ks = jax.random.split(key, 32)
    nrm = lambda k, shape, scale: jax.random.normal(k, shape, jnp.float32) * scale
    n_pages = PAST_LEN // PAGE_SIZE
    n_used = DEC_BATCH * n_pages
    n_pool = n_used + max(1, n_used // 4)
    win_buf = min(WINDOW, PAST_LEN)
    page_table = jax.random.permutation(ks[0], n_pool)[:n_used].reshape(DEC_BATCH, n_pages).astype(jnp.int32)
    return {
        'x_prompt': nrm(ks[1], (BATCH, SEQ, D_MODEL), 1.0),
        'x_sample': nrm(ks[2], (DEC_BATCH, DEC_SEQ, D_MODEL), 1.0),
        'cache_cmp_kv': nrm(ks[3], (N_ATTN_LAYERS, n_pool, PAGE_SIZE, 2, NSA_KV_HEADS, HEAD_DIM), 1.0),
        'cache_sel_kv': nrm(ks[4], (N_ATTN_LAYERS, n_pool, PAGE_SIZE, 2, NSA_KV_HEADS, HEAD_DIM), 1.0),
        'cache_win_kv': nrm(ks[5], (N_ATTN_LAYERS, DEC_BATCH, win_buf, 2, NSA_KV_HEADS, HEAD_DIM), 1.0),
        'cache_sb_kv': nrm(ks[6], (N_ATTN_LAYERS, n_pool, PAGE_SIZE, 2, SB_HEADS, HEAD_DIM), 1.0),
        'page_table': page_table,
        'rel_bias': nrm(ks[7], (N_BUCKETS, NSA_HEADS), 0.5),
        'mix_norm': 1.0 + nrm(ks[8], (DEPTH, D_MODEL), 0.05),
        'ffn_norm': 1.0 + nrm(ks[9], (DEPTH, D_MODEL), 0.05),
        'final_norm': 1.0 + nrm(ks[10], (D_MODEL,), 0.05),
        'a_w_in': nrm(ks[11], (N_ATTN_LAYERS, D_MODEL, EVEN_IN), D_MODEL ** -0.5),
        'a_w_out': nrm(ks[12], (N_ATTN_LAYERS, MIX_WIDTH, D_MODEL), MIX_WIDTH ** -0.5),
        'cmp_pe': nrm(ks[13], (N_ATTN_LAYERS, 2, CMP_BLOCK, HEAD_DIM), 0.1),
        'cmp_w1': nrm(ks[14], (N_ATTN_LAYERS, 2, CMP_BLOCK * HEAD_DIM, CMP_HIDDEN), (CMP_BLOCK * HEAD_DIM) ** -0.5),
        'cmp_w2': nrm(ks[15], (N_ATTN_LAYERS, 2, CMP_HIDDEN, HEAD_DIM), CMP_HIDDEN ** -0.5),
        'c_w_in': nrm(ks[16], (N_CHUNK_LAYERS, D_MODEL, 2 * C_HALF), D_MODEL ** -0.5),
        'c_w_s': nrm(ks[17], (N_CHUNK_LAYERS, C_GROUPS, CHUNK, CHUNK), CHUNK ** -0.5),
        'c_b_s': 1.0 + nrm(ks[18], (N_CHUNK_LAYERS, C_GROUPS, CHUNK), 0.1),
        'c_ln_g': 1.0 + nrm(ks[19], (N_CHUNK_LAYERS, C_HALF), 0.05),
        'c_ln_b': nrm(ks[20], (N_CHUNK_LAYERS, C_HALF), 0.02),
        'c_w_out': nrm(ks[21], (N_CHUNK_LAYERS, C_HALF, D_MODEL), C_HALF ** -0.5),
        'peer_w_q': nrm(ks[22], (DEPTH, D_MODEL, PEER_HEADS * D_KEY), D_MODEL ** -0.5),
        'peer_keys': nrm(ks[23], (DEPTH, PEER_HEADS, 2, N_KEYS, D_KEY // 2), (D_KEY // 2) ** -0.5),
        'peer_u': nrm(ks[24], (DEPTH, N_EXPERTS, D_MODEL), D_MODEL ** -0.5),
        'peer_v': nrm(ks[25], (DEPTH, N_EXPERTS, D_MODEL), (PEER_HEADS * PEER_TOPK) ** -0.5),
    }


def reference(x_prompt, x_sample, cache_cmp_kv, cache_sel_kv, cache_win_kv, cache_sb_kv, page_table,
              rel_bias, mix_norm, ffn_norm, final_norm, a_w_in, a_w_out, cmp_pe, cmp_w1, cmp_w2,
              c_w_in, c_w_s, c_b_s, c_ln_g, c_ln_b, c_w_out, peer_w_q, peer_keys, peer_u, peer_v):
    past_len = page_table.shape[1] * cache_cmp_kv.shape[2]
    yp, ys = x_prompt, x_sample
    p_cmp, p_sel, p_win, p_sb = [], [], [], []
    s_cmp, s_sel, s_win, s_sb, s_chunk = [], [], [], [], []
    for layer in range(DEPTH):
        i = layer // 2
        hp = rms_norm(yp, mix_norm[layer])
        hs = rms_norm(ys, mix_norm[layer])
        if layer % 2 == 0:
            w = (a_w_in[i], a_w_out[i], cmp_pe[i], cmp_w1[i], cmp_w2[i], rel_bias)
            out_p, st_p = attn_mixer(hp, *w, None, 0)
            past = (gather_pages(cache_cmp_kv[i], page_table), gather_pages(cache_sel_kv[i], page_table),
                    cache_win_kv[i], gather_pages(cache_sb_kv[i], page_table))
            out_s, st_s = attn_mixer(hs, *w, past, past_len)
            p_cmp.append(st_p[0]); p_sel.append(st_p[1]); p_win.append(st_p[2]); p_sb.append(st_p[3])
            s_cmp.append(st_s[0]); s_sel.append(st_s[1]); s_win.append(st_s[2]); s_sb.append(st_s[3])
        else:
            w = (c_w_in[i], c_w_s[i], c_b_s[i], c_ln_g[i], c_ln_b[i], c_w_out[i])
            out_p, _ = chunk_mixer(hp, *w)
            out_s, v_s = chunk_mixer(hs, *w)
            s_chunk.append(v_s)
        yp = yp + out_p
        ys = ys + out_s
        pw = (peer_w_q[layer], peer_keys[layer], peer_u[layer], peer_v[layer])
        yp = yp + peer(rms_norm(yp, ffn_norm[layer]), *pw)
        ys = ys + peer(rms_norm(ys, ffn_norm[layer]), *pw)
    y_prompt = rms_norm(yp, final_norm)
    y_sample = rms_norm(ys, final_norm)
    prompt_cmp_kv = jnp.stack(p_cmp)
    prompt_sel_kv = jnp.stack(p_sel)
    prompt_win_kv = jnp.stack(p_win)
    prompt_sb_kv = jnp.stack(p_sb)
    sample_cmp_kv = jnp.stack(s_cmp)
    sample_sel_kv = jnp.stack(s_sel)
    sample_win_kv = jnp.stack(s_win)
    sample_sb_kv = jnp.stack(s_sb)
    sample_chunk_v = jnp.stack(s_chunk)
    return (y_prompt, y_sample, prompt_cmp_kv, prompt_sel_kv, prompt_win_kv, prompt_sb_kv,
            sample_cmp_kv, sample_sel_kv, sample_win_kv, sample_sb_kv, sample_chunk_v)
```

```python
import functools
import math

import numpy as np
import jax
import jax.numpy as jnp
from jax import lax
from jax.experimental import pallas as pl
from jax.experimental.pallas import tpu as pltpu

F32 = jnp.float32
BF16 = jnp.bfloat16
I32 = jnp.int32

D_MODEL = 1024
HEAD_DIM = 64
NSA_HEADS = 8
NSA_KV_HEADS = 2
NSA_GROUP = 4
SB_HEADS = 8
N_BRANCH = 3
CMP_BLOCK = 32
CMP_STRIDE = 16
CMP_HIDDEN = 128
SEL_BLOCK = 64
N_SEL = 16
WINDOW = 512
N_BUCKETS = 32
MAX_DISTANCE = 128
CHUNK = 128
C_GROUPS = 8
N_KEYS = 128
PEER_HEADS = 8
PEER_TOPK = 16
EPS = 1e-6
NEG = -1e30
FORCE_SCORE = 1e4
SCALE = HEAD_DIM ** -0.5

LANE = 128
ROW_TILE = 256
VMEM_LIMIT = 56 * 1024 * 1024

NT_DIMS = (((1,), (1,)), ((), ()))


def _cparams(*sem):
    return pltpu.CompilerParams(dimension_semantics=sem, vmem_limit_bytes=VMEM_LIMIT)


def _rms(x, g):
    return x * lax.rsqrt(jnp.mean(x * x, axis=-1, keepdims=True) + EPS) * g


def _gelu(x):
    return 0.5 * x * (1.0 + jnp.tanh(math.sqrt(2.0 / math.pi) * (x + 0.044715 * (x * x * x))))


def _proj_body(y_ref, g_ref, w_ref, *out_refs, outs):
    h = _rms(y_ref[...], g_ref[...]).astype(BF16)
    for (a, b, _, scale), o_ref in zip(outs, out_refs):
        acc = jnp.dot(h, w_ref[:, a:b], preferred_element_type=F32)
        if scale != 1.0:
            acc = acc * scale
        o_ref[...] = acc.astype(o_ref.dtype)


def _rms_proj(y, gamma, w_bf16, outs):
    ntp, d = y.shape
    n = w_bf16.shape[1]
    grid = (ntp // ROW_TILE,)
    return pl.pallas_call(
        functools.partial(_proj_body, outs=outs),
        grid=grid,
        in_specs=[pl.BlockSpec((ROW_TILE, d), lambda i: (i, 0)),
                  pl.BlockSpec((1, d), lambda i: (0, 0)),
                  pl.BlockSpec((d, n), lambda i: (0, 0))],
        out_specs=[pl.BlockSpec((ROW_TILE, b - a), lambda i: (i, 0)) for a, b, _, _ in outs],
        out_shape=[jax.ShapeDtypeStruct((ntp, b - a), dt) for a, b, dt, _ in outs],
        compiler_params=_cparams("parallel"),
        name="rms_proj",
    )(y, gamma.reshape(1, d), w_bf16)


def _peer_score_body(y_ref, g_ref, wq_ref, keys_ref, st_ref):
    h = _rms(y_ref[...], g_ref[...]).astype(BF16)
    for hc in range(2 * PEER_HEADS):
        q = jnp.dot(h, wq_ref[:, hc * LANE:(hc + 1) * LANE], preferred_element_type=F32).astype(BF16)
        st_ref[hc * N_KEYS:(hc + 1) * N_KEYS, :] = lax.dot_general(
            keys_ref[hc], q, NT_DIMS, preferred_element_type=F32)


def _peer_scores(y, gamma, wq_bf16, keys_bf16):
    ntp, d = y.shape
    nq = wq_bf16.shape[1]
    return pl.pallas_call(
        _peer_score_body,
        grid=(ntp // ROW_TILE,),
        in_specs=[pl.BlockSpec((ROW_TILE, d), lambda i: (i, 0)),
                  pl.BlockSpec((1, d), lambda i: (0, 0)),
                  pl.BlockSpec((d, nq), lambda i: (0, 0)),
                  pl.BlockSpec(keys_bf16.shape, lambda i: (0, 0, 0))],
        out_specs=pl.BlockSpec((nq, ROW_TILE), lambda i: (0, i)),
        out_shape=jax.ShapeDtypeStruct((nq, ntp), F32),
        compiler_params=_cparams("parallel"),
        name="peer_scores",
    )(y, gamma.reshape(1, d), wq_bf16, keys_bf16)


def _topk_rows(x, k):
    n = x.shape[0]
    iota = lax.broadcasted_iota(I32, x.shape, 0)
    vals, idxs = [], []
    for _ in range(k):
        m = jnp.max(x, axis=0, keepdims=True)
        idx = jnp.min(jnp.where(x == m, iota, n), axis=0, keepdims=True)
        vals.append(m)
        idxs.append(idx)
        x = jnp.where(iota == idx, -jnp.inf, x)
    return jnp.concatenate(vals, axis=0), jnp.concatenate(idxs, axis=0)


def _pick_rows(table, idx):
    out = jnp.zeros(idx.shape, table.dtype)
    for a in range(table.shape[0]):
        out = jnp.where(idx == a, table[a:a + 1, :], out)
    return out


def _peer_topk_body(st_ref, i1_ref, i2_ref, g_ref):
    k = PEER_TOPK

    def head(h, carry):
        base = pl.multiple_of(h * (2 * N_KEYS), 2 * N_KEYS)
        s1, i1 = _topk_rows(st_ref[pl.ds(base, N_KEYS), :], k)
        s2, i2 = _topk_rows(st_ref[pl.ds(base + N_KEYS, N_KEYS), :], k)
        cand = jnp.concatenate([s1[a:a + 1, :] + s2 for a in range(k)], axis=0)
        top_s, top_j = _topk_rows(cand, k)
        e = jnp.exp(top_s - top_s[0:1, :])
        gate = e / jnp.sum(e, axis=0, keepdims=True)
        row = pl.multiple_of(h * k, k)
        i1_ref[pl.ds(row, k), :] = _pick_rows(i1, top_j >> 4)
        i2_ref[pl.ds(row, k), :] = _pick_rows(i2, top_j & (k - 1))
        g_ref[pl.ds(row, k), :] = gate
        return carry

    lax.fori_loop(0, PEER_HEADS, head, 0)


def _peer_topk(st):
    nq, ntp = st.shape
    slots = PEER_HEADS * PEER_TOPK
    spec = pl.BlockSpec((slots, ROW_TILE), lambda i: (0, i))
    return pl.pallas_call(
        _peer_topk_body,
        grid=(ntp // ROW_TILE,),
        in_specs=[pl.BlockSpec((nq, ROW_TILE), lambda i: (0, i))],
        out_specs=[spec, spec, spec],
        out_shape=[jax.ShapeDtypeStruct((slots, ntp), I32),
                   jax.ShapeDtypeStruct((slots, ntp), I32),
                   jax.ShapeDtypeStruct((slots, ntp), F32)],
        compiler_params=_cparams("parallel"),
        name="peer_topk",
    )(st)


PEER_TM = 256
PEER_CHUNK = 2


def _peer_dense_body(y_ref, g_ref, i1_ref, i2_ref, gate_ref, u_ref, v_ref, o_ref, w_scr, xh_scr):
    c = pl.program_id(1)
    tm = y_ref.shape[0]

    @pl.when(c == 0)
    def _():
        y = y_ref[...]
        o_ref[...] = y
        xh_scr[...] = _rms(y, g_ref[...]).astype(BF16)
        sub = lax.broadcasted_iota(I32, (N_KEYS, LANE), 0)

        def token(t, carry):
            i1 = i1_ref[pl.ds(t, 1), :]
            i2 = i2_ref[pl.ds(t, 1), :]
            gate = gate_ref[pl.ds(t, 1), :]
            g_hi = gate.astype(BF16).astype(F32)
            g_lo = gate - g_hi
            a_t = jnp.where(sub == i1, 1.0, 0.0).astype(BF16)
            hit2 = sub == i2
            b_hi = jnp.where(hit2, g_hi, 0.0).astype(BF16)
            b_lo = jnp.where(hit2, g_lo, 0.0).astype(BF16)
            w = lax.dot_general(jnp.concatenate([a_t, a_t], axis=1),
                                jnp.concatenate([b_hi, b_lo], axis=1),
                                NT_DIMS, preferred_element_type=F32)
            w_scr[pl.ds(pl.multiple_of(t * N_KEYS, N_KEYS), N_KEYS), :] = w
            return carry

        lax.fori_loop(0, tm, token, 0)

    act = _gelu(lax.dot_general(xh_scr[...], u_ref[...], NT_DIMS, preferred_element_type=F32))
    w_c = jnp.concatenate(
        [w_scr[pl.ds(c * PEER_CHUNK + j, tm, stride=N_KEYS), :] for j in range(PEER_CHUNK)], axis=1)
    o_ref[...] += jnp.dot((w_c * act).astype(BF16), v_ref[...], preferred_element_type=F32)


def _peer_dense(y, gamma, i1, i2, gate, u_bf16, v_bf16):
    ntp, d = y.shape
    n_exp = u_bf16.shape[0]
    ce = PEER_CHUNK * N_KEYS
    slots = i1.shape[1]
    tok = lambda i, c: (i, 0)
    return pl.pallas_call(
        _peer_dense_body,
        grid=(ntp // PEER_TM, n_exp // ce),
        in_specs=[pl.BlockSpec((PEER_TM, d), tok),
                  pl.BlockSpec((1, d), lambda i, c: (0, 0)),
                  pl.BlockSpec((PEER_TM, slots), tok),
                  pl.BlockSpec((PEER_TM, slots), tok),
                  pl.BlockSpec((PEER_TM, slots), tok),
                  pl.BlockSpec((ce, d), lambda i, c: (c, 0)),
                  pl.BlockSpec((ce, d), lambda i, c: (c, 0))],
        out_specs=pl.BlockSpec((PEER_TM, d), tok),
        out_shape=jax.ShapeDtypeStruct((ntp, d), F32),
        scratch_shapes=[pltpu.VMEM((PEER_TM * N_KEYS, N_KEYS), F32),
                        pltpu.VMEM((PEER_TM, d), BF16)],
        compiler_params=_cparams("parallel", "arbitrary"),
        name="peer_dense",
    )(y, gamma.reshape(1, d), i1, i2, gate, u_bf16, v_bf16)


def _peer_layer(y, gamma, wq_bf16, keys_bf16, u_bf16, v_bf16):
    st = _peer_scores(y, gamma, wq_bf16, keys_bf16)
    i1, i2, gate = _peer_topk(st)
    return _peer_dense(y, gamma, i1.T, i2.T, gate.T, u_bf16, v_bf16)


def _chunk_body(y_ref, gm_ref, win_ref, ws_ref, bs_ref, lng_ref, lnb_ref, wout_ref, o_ref, v_ref, *, n_prompt_tiles):
    i = pl.program_id(0)
    c_half = lng_ref.shape[1]
    gd = c_half // C_GROUPS
    y = y_ref[...]
    h = _rms(y, gm_ref[...]).astype(BF16)
    u = _gelu(jnp.dot(h, win_ref[:, :c_half], preferred_element_type=F32))
    v = _gelu(jnp.dot(h, win_ref[:, c_half:], preferred_element_type=F32))
    vc = v - jnp.mean(v, axis=-1, keepdims=True)
    vn = vc * lax.rsqrt(jnp.mean(vc * vc, axis=-1, keepdims=True) + EPS) * lng_ref[...] + lnb_ref[...]
    v_ref[...] = vn
    vb = vn.astype(BF16)
    row = lax.broadcasted_iota(I32, (CHUNK, CHUNK), 0)
    col = lax.broadcasted_iota(I32, (CHUNK, CHUNK), 1)
    is_sample = i >= n_prompt_tiles
    mixed = []
    for g in range(C_GROUPS):
        ws = ws_ref[g]
        w_prompt = jnp.where(row >= col, ws, 0.0)
        w_sample = jnp.where(row == col, ws[0:1, 0:1], 0.0)
        wg = jnp.where(is_sample, w_sample, w_prompt).astype(BF16)
        mixed.append(jnp.concatenate(
            [jnp.dot(wg, vb[ch * CHUNK:(ch + 1) * CHUNK, g * gd:(g + 1) * gd], preferred_element_type=F32)
             for ch in range(ROW_TILE // CHUNK)], axis=0))
    bias = jnp.concatenate([bs_ref[0]] * (ROW_TILE // CHUNK), axis=0)
    gated = (u * (jnp.concatenate(mixed, axis=1) + bias)).astype(BF16)
    o_ref[...] = y + jnp.dot(gated, wout_ref[...], preferred_element_type=F32)


def _chunk_layer(y, gamma, w_in_bf16, w_s, bs_map, ln_g, ln_b, w_out_bf16, n_prompt_rows):
    ntp, d = y.shape
    c_half = w_out_bf16.shape[0]
    npt = n_prompt_rows // ROW_TILE
    const2 = lambda i: (0, 0)
    return pl.pallas_call(
        functools.partial(_chunk_body, n_prompt_tiles=npt),
        grid=(ntp // ROW_TILE,),
        in_specs=[pl.BlockSpec((ROW_TILE, d), lambda i: (i, 0)),
                  pl.BlockSpec((1, d), const2),
                  pl.BlockSpec(w_in_bf16.shape, const2),
                  pl.BlockSpec(w_s.shape, lambda i: (0, 0, 0)),
                  pl.BlockSpec((1, CHUNK, c_half), lambda i: (jnp.where(i >= npt, 1, 0), 0, 0)),
                  pl.BlockSpec((1, c_half), const2),
                  pl.BlockSpec((1, c_half), const2),
                  pl.BlockSpec(w_out_bf16.shape, const2)],
        out_specs=[pl.BlockSpec((ROW_TILE, d), lambda i: (i, 0)),
                   pl.BlockSpec((ROW_TILE, c_half), lambda i: (i, 0))],
        out_shape=[jax.ShapeDtypeStruct((ntp, d), F32), jax.ShapeDtypeStruct((ntp, c_half), F32)],
        compiler_params=_cparams("parallel"),
        name="chunk_layer",
    )(y, gamma.reshape(1, d), w_in_bf16, w_s, bs_map, ln_g.reshape(1, -1), ln_b.reshape(1, -1), w_out_bf16)


def _attn_out_body(y_ref, ocmp_ref, osel_ref, owin_ref, osb_ref, gl_ref, ex_ref, wout_ref, o_ref):
    nsa_w = ocmp_ref.shape[1]
    gates = 1.0 / (1.0 + jnp.exp(-gl_ref[...]))
    o_nsa = jnp.zeros(ocmp_ref.shape, F32)
    for j, br_ref in enumerate((ocmp_ref, osel_ref, owin_ref)):
        gj = jnp.dot(gates, ex_ref[j], preferred_element_type=F32, precision=lax.Precision.HIGHEST)
        o_nsa = o_nsa + gj * br_ref[...]
    acc = jnp.dot(o_nsa.astype(BF16), wout_ref[:nsa_w, :], preferred_element_type=F32)
    acc = acc + jnp.dot(osb_ref[...].astype(BF16), wout_ref[nsa_w:, :], preferred_element_type=F32)
    o_ref[...] = y_ref[...] + acc


def _gate_expand():
    ex = np.zeros((N_BRANCH, LANE, NSA_HEADS * HEAD_DIM), np.float32)
    for j in range(N_BRANCH):
        for h in range(NSA_HEADS):
            ex[j, h * N_BRANCH + j, h * HEAD_DIM:(h + 1) * HEAD_DIM] = 1.0
    return jnp.asarray(ex)


def _attn_out(y, ocmp, osel, owin, osb, gl, w_out_bf16):
    ntp, d = y.shape
    ex = _gate_expand()
    row = lambda w: pl.BlockSpec((ROW_TILE, w), lambda i: (i, 0))
    return pl.pallas_call(
        _attn_out_body,
        grid=(ntp // ROW_TILE,),
        in_specs=[row(d), row(ocmp.shape[1]), row(osel.shape[1]), row(owin.shape[1]), row(osb.shape[1]),
                  row(gl.shape[1]),
                  pl.BlockSpec(ex.shape, lambda i: (0, 0, 0)),
                  pl.BlockSpec(w_out_bf16.shape, lambda i: (0, 0))],
        out_specs=row(d),
        out_shape=jax.ShapeDtypeStruct((ntp, d), F32),
        compiler_params=_cparams("parallel"),
        name="attn_out",
    )(y, ocmp, osel, owin, osb, gl, ex, w_out_bf16)


def _final_norm_body(y_ref, g_ref, o_ref):
    o_ref[...] = _rms(y_ref[...], g_ref[...])


def _final_norm(y, gamma):
    ntp, d = y.shape
    return pl.pallas_call(
        _final_norm_body,
        grid=(ntp // ROW_TILE,),
        in_specs=[pl.BlockSpec((ROW_TILE, d), lambda i: (i, 0)), pl.BlockSpec((1, d), lambda i: (0, 0))],
        out_specs=pl.BlockSpec((ROW_TILE, d), lambda i: (i, 0)),
        out_shape=jax.ShapeDtypeStruct((ntp, d), F32),
        compiler_params=_cparams("parallel"),
        name="final_norm",
    )(y, gamma.reshape(1, d))


def _t5_bucket(dist):
    n = jnp.maximum(dist, 0)
    exact = N_BUCKETS // 2
    nf = jnp.maximum(n, exact).astype(F32)
    large = exact + (jnp.log(nf / exact) / math.log(MAX_DISTANCE / exact) * (N_BUCKETS - exact)).astype(I32)
    return jnp.where(n < exact, n, jnp.minimum(large, N_BUCKETS - 1))


def _bias_tables_body(rb_ref, cb_ref, toep_ref, dsel_ref, dmisc_ref, *, t_len, past_len):
    h = pl.program_id(0)

    def lookup(dist):
        bucket = _t5_bucket(dist)
        out = jnp.zeros(dist.shape, F32)
        for k in range(N_BUCKETS):
            out = jnp.where(bucket == k, rb_ref[k, h], out)
        return out

    t = lax.broadcasted_iota(I32, (t_len, LANE), 0)
    c = lax.broadcasted_iota(I32, (t_len, LANE), 1)
    cb_ref[0] = lookup(t - (c * CMP_STRIDE + CMP_BLOCK - 1))
    i = lax.broadcasted_iota(I32, (LANE, LANE), 0)
    j = lax.broadcasted_iota(I32, (LANE, LANE), 1)
    toep_ref[0, 0] = lookup(i - j)
    toep_ref[1, 0] = lookup(LANE + i - j)
    toep_ref[2, 0] = lookup(jnp.full((LANE, LANE), 2 * MAX_DISTANCE, I32))
    s = lax.broadcasted_iota(I32, (8, past_len), 1)
    dsel_ref[0] = lookup(past_len - s)
    c2 = lax.broadcasted_iota(I32, (8, 2 * LANE), 1)
    dmisc_ref[0] = lookup(jnp.where(c2 < LANE, past_len - (c2 * CMP_STRIDE + CMP_BLOCK - 1), 0))


def _bias_tables(rel_bias, t_len, past_len):
    nh = rel_bias.shape[1]
    return pl.pallas_call(
        functools.partial(_bias_tables_body, t_len=t_len, past_len=past_len),
        grid=(nh,),
        in_specs=[pl.BlockSpec(memory_space=pltpu.SMEM)],
        out_specs=[pl.BlockSpec((1, t_len, LANE), lambda h: (h, 0, 0)),
                   pl.BlockSpec((3, 1, LANE, LANE), lambda h: (0, h, 0, 0)),
                   pl.BlockSpec((1, 8, past_len), lambda h: (h, 0, 0)),
                   pl.BlockSpec((1, 8, 2 * LANE), lambda h: (h, 0, 0))],
        out_shape=[jax.ShapeDtypeStruct((nh, t_len, LANE), F32),
                   jax.ShapeDtypeStruct((3, nh, LANE, LANE), F32),
                   jax.ShapeDtypeStruct((nh, 8, past_len), F32),
                   jax.ShapeDtypeStruct((nh, 8, 2 * LANE), F32)],
        compiler_params=_cparams("parallel"),
        name="bias_tables",
    )(rel_bias)


GROUP_ROWS = CMP_STRIDE
KV_ROW = 2 * NSA_KV_HEADS * HEAD_DIM


def _compress_weights(cmp_pe, cmp_w1, cmp_w2):
    eye2 = jnp.eye(2, dtype=F32)
    w1r = cmp_w1.reshape(2, CMP_BLOCK, HEAD_DIM, CMP_HIDDEN)
    halves = []
    for half in range(2):
        w = jnp.einsum('krdn,kK,gG->rKGdkgn', w1r[:, half * GROUP_ROWS:(half + 1) * GROUP_ROWS], eye2, eye2)
        halves.append(w.reshape(GROUP_ROWS * KV_ROW, 4 * CMP_HIDDEN).astype(BF16))
    pe = jnp.broadcast_to(cmp_pe.reshape(2, 2, GROUP_ROWS, 1, HEAD_DIM), (2, 2, GROUP_ROWS, NSA_KV_HEADS, HEAD_DIM))
    pe = jnp.transpose(pe, (1, 2, 0, 3, 4)).reshape(2, 1, GROUP_ROWS * KV_ROW)
    w2 = jnp.einsum('knd,kK,gG->kgnKGd', cmp_w2, eye2, eye2).reshape(4 * CMP_HIDDEN, KV_ROW).astype(BF16)
    return halves[0], halves[1], pe, w2


def _compress_math(x, pe_ref, w1a_ref, w1b_ref, w2_ref):
    fa = jnp.dot((x + pe_ref[0]).astype(BF16), w1a_ref[...], preferred_element_type=F32)
    fb = jnp.dot((x + pe_ref[1]).astype(BF16), w1b_ref[...], preferred_element_type=F32)
    pre = fa + pltpu.roll(fb, fb.shape[0] - 1, axis=0)
    return jnp.dot(_gelu(pre).astype(BF16), w2_ref[...], preferred_element_type=F32)


def _compress_body(x_ref, pe_ref, w1a_ref, w1b_ref, w2_ref, o_ref):
    o_ref[...] = _compress_math(x_ref[...], pe_ref, w1a_ref, w1b_ref, w2_ref)


def _compress(x_groups, n_batch, cw):
    w1a, w1b, pe, w2 = cw
    g = LANE
    width = x_groups.shape[1]
    return pl.pallas_call(
        _compress_body,
        grid=(n_batch,),
        in_specs=[pl.BlockSpec((g, width), lambda b: (b, 0)),
                  pl.BlockSpec(pe.shape, lambda b: (0, 0, 0)),
                  pl.BlockSpec(w1a.shape, lambda b: (0, 0)),
                  pl.BlockSpec(w1b.shape, lambda b: (0, 0)),
                  pl.BlockSpec(w2.shape, lambda b: (0, 0))],
        out_specs=pl.BlockSpec((g, KV_ROW), lambda b: (b, 0)),
        out_shape=jax.ShapeDtypeStruct((n_batch * g, KV_ROW), F32),
        compiler_params=_cparams("parallel"),
        name="compress",
    )(x_groups, pe, w1a, w1b, w2)


TQ = 128
NSA_W = NSA_HEADS * HEAD_DIM
KVW = NSA_KV_HEADS * HEAD_DIM


def _overlap_matrix(n_cmp, n_sb):
    c0 = np.arange(LANE) * CMP_STRIDE
    j0 = np.arange(LANE) * SEL_BLOCK
    ov = (c0[:, None] < j0[None, :] + SEL_BLOCK) & (c0[:, None] + CMP_BLOCK > j0[None, :])
    ov &= (np.arange(LANE)[:, None] < n_cmp) & (np.arange(LANE)[None, :] < n_sb)
    return jnp.asarray(ov.astype(np.float32))


def _select_mask(score, tpos, n_sb):
    j = lax.broadcasted_iota(I32, score.shape, 1)
    cur = tpos // SEL_BLOCK
    forced = (j == 0) | (j == cur) | (j == cur - 1)
    score = jnp.where(forced, FORCE_SCORE, score)
    score = jnp.where(j * SEL_BLOCK <= tpos, score, NEG)
    score = jnp.where(j < n_sb, score, -3e38)
    ahead = jnp.zeros(score.shape, F32)
    for i in range(n_sb):
        ci = score[:, i:i + 1]
        ahead = ahead + jnp.where((ci > score) | ((ci == score) & (i < j)), 1.0, 0.0)
    return jnp.where(ahead < min(N_SEL, n_sb), 1.0, 0.0)


def _cmp_select_body(q_ref, kvc_ref, cb_ref, ov_ref, ocmp_ref, selm_ref, *, n_cmp, n_sb):
    qb = pl.program_id(1)
    tq = q_ref.shape[0]
    tpos = qb * tq + lax.broadcasted_iota(I32, (tq, LANE), 0)
    c = lax.broadcasted_iota(I32, (tq, LANE), 1)
    vis = (c * CMP_STRIDE + (CMP_BLOCK - 1) <= tpos) & (c < n_cmp)
    kvc = kvc_ref[...].astype(BF16)
    for g in range(NSA_KV_HEADS):
        kc = kvc[:, g * HEAD_DIM:(g + 1) * HEAD_DIM]
        vc = kvc[:, KVW + g * HEAD_DIM:KVW + (g + 1) * HEAD_DIM]
        p_sum = jnp.zeros((tq, LANE), F32)
        for r in range(NSA_GROUP):
            h = g * NSA_GROUP + r
            s = lax.dot_general(q_ref[:, h * HEAD_DIM:(h + 1) * HEAD_DIM], kc, NT_DIMS, preferred_element_type=F32)
            s = jnp.where(vis, s + cb_ref[h], NEG)
            e = jnp.exp(s - jnp.max(s, axis=-1, keepdims=True))
            p = jnp.where(vis, e / jnp.sum(e, axis=-1, keepdims=True), 0.0)
            ocmp_ref[:, h * HEAD_DIM:(h + 1) * HEAD_DIM] = jnp.dot(p.astype(BF16), vc, preferred_element_type=F32)
            p_sum = p_sum + p
        score = jnp.dot(p_sum, ov_ref[...], preferred_element_type=F32, precision=lax.Precision.HIGHEST)
        selm_ref[:, g * LANE:(g + 1) * LANE] = _select_mask(score, tpos, n_sb)


def _cmp_select(qb16, kvc, cb, n_batch, t_len):
    n_cmp = (t_len - CMP_BLOCK) // CMP_STRIDE + 1
    n_sb = -(-t_len // SEL_BLOCK)
    ov = _overlap_matrix(n_cmp, n_sb)
    nq = t_len // TQ
    rows = n_batch * t_len
    return pl.pallas_call(
        functools.partial(_cmp_select_body, n_cmp=n_cmp, n_sb=n_sb),
        grid=(n_batch, nq),
        in_specs=[pl.BlockSpec((TQ, NSA_W), lambda b, i: (b * nq + i, 0)),
                  pl.BlockSpec((LANE, KV_ROW), lambda b, i: (b, 0)),
                  pl.BlockSpec((NSA_HEADS, TQ, LANE), lambda b, i: (0, i, 0)),
                  pl.BlockSpec((LANE, LANE), lambda b, i: (0, 0))],
        out_specs=[pl.BlockSpec((TQ, NSA_W), lambda b, i: (b * nq + i, 0)),
                   pl.BlockSpec((TQ, NSA_KV_HEADS * LANE), lambda b, i: (b * nq + i, 0))],
        out_shape=[jax.ShapeDtypeStruct((rows, NSA_W), F32),
                   jax.ShapeDtypeStruct((rows, NSA_KV_HEADS * LANE), F32)],
        compiler_params=_cparams("parallel", "parallel"),
        name="cmp_select",
    )(qb16, kvc, cb, ov)


def _gqa_flash(q_ref, kv_ref, toep_ref, o_ref, g, kb_lo, kb_hi, qb, allowed_fn):
    tq = q_ref.shape[0]
    rows = NSA_GROUP * tq
    q = jnp.concatenate([q_ref[:, (g * NSA_GROUP + r) * HEAD_DIM:(g * NSA_GROUP + r + 1) * HEAD_DIM]
                         for r in range(NSA_GROUP)], axis=0)
    tpos = qb * tq + lax.broadcasted_iota(I32, (tq, LANE), 0)

    def step(kb, carry):
        m, l, acc = carry
        kv = kv_ref[kb]
        k = kv[:, g * HEAD_DIM:(g + 1) * HEAD_DIM]
        v = kv[:, KVW + g * HEAD_DIM:KVW + (g + 1) * HEAD_DIM]
        kpos = kb * LANE + lax.broadcasted_iota(I32, (tq, LANE), 1)
        ok = allowed_fn(kb, tpos, kpos)
        bias = toep_ref[jnp.minimum(qb - kb, 2), pl.ds(g * NSA_GROUP, NSA_GROUP)]
        s = lax.dot_general(q, k, NT_DIMS, preferred_element_type=F32).reshape(NSA_GROUP, tq, LANE) + bias
        s = jnp.where(ok[None], s, NEG).reshape(rows, LANE)
        m_new = jnp.maximum(m, jnp.max(s, axis=-1, keepdims=True))
        alpha = jnp.exp(m - m_new)
        p = jnp.exp(s - m_new)
        l = alpha * l + jnp.sum(p, axis=-1, keepdims=True)
        acc = alpha * acc + jnp.dot(p.astype(BF16), v, preferred_element_type=F32)
        return m_new, l, acc

    init = (jnp.full((rows, 1), NEG, F32), jnp.zeros((rows, 1), F32), jnp.zeros((rows, HEAD_DIM), F32))
    m, l, acc = lax.fori_loop(kb_lo, kb_hi + 1, step, init)
    out = acc / l
    for r in range(NSA_GROUP):
        h = g * NSA_GROUP + r
        o_ref[:, h * HEAD_DIM:(h + 1) * HEAD_DIM] = out[r * tq:(r + 1) * tq]


def _sel_body(q_ref, kv_ref, selm_ref, ex_ref, toep_ref, o_ref):
    qb = pl.program_id(1)
    for g in range(NSA_KV_HEADS):
        selm = selm_ref[:, g * LANE:(g + 1) * LANE].astype(BF16)

        def allowed(kb, tpos, kpos, selm=selm):
            chosen = jnp.dot(selm, ex_ref[kb], preferred_element_type=F32)
            return (chosen > 0.5) & (kpos <= tpos)

        _gqa_flash(q_ref, kv_ref, toep_ref, o_ref, g, 0, qb, qb, allowed)


def _win_body(q_ref, kv_ref, toep_ref, o_ref):
    qb = pl.program_id(1)

    def allowed(kb, tpos, kpos):
        dist = tpos - kpos
        return (dist >= 0) & (dist < WINDOW)

    for g in range(NSA_KV_HEADS):
        _gqa_flash(q_ref, kv_ref, toep_ref, o_ref, g, jnp.maximum(qb - WINDOW // LANE, 0), qb, qb, allowed)


def _block_expand(n_blocks):
    kb = np.arange(n_blocks)[:, None, None]
    j = np.arange(LANE)[None, :, None]
    s = np.arange(LANE)[None, None, :]
    return jnp.asarray(((kb * LANE + s) // SEL_BLOCK == j).astype(np.float32), dtype=BF16)


def _sel_attn(qb16, selkv16, selm, toep, n_batch, t_len):
    nq = t_len // TQ
    ex = _block_expand(nq)
    kv3 = selkv16.reshape(-1, LANE, KV_ROW)
    return pl.pallas_call(
        _sel_body,
        grid=(n_batch, nq),
        in_specs=[pl.BlockSpec((TQ, NSA_W), lambda b, i: (b * nq + i, 0)),
                  pl.BlockSpec((nq, LANE, KV_ROW), lambda b, i: (b, 0, 0)),
                  pl.BlockSpec((TQ, NSA_KV_HEADS * LANE), lambda b, i: (b * nq + i, 0)),
                  pl.BlockSpec(ex.shape, lambda b, i: (0, 0, 0)),
                  pl.BlockSpec(toep.shape, lambda b, i: (0, 0, 0, 0))],
        out_specs=pl.BlockSpec((TQ, NSA_W), lambda b, i: (b * nq + i, 0)),
        out_shape=jax.ShapeDtypeStruct((n_batch * t_len, NSA_W), F32),
        compiler_params=_cparams("parallel", "parallel"),
        name="sel_attn",
    )(qb16, kv3, selm, ex, toep)


def _win_attn(qb16, winkv16, toep, n_batch, t_len):
    nq = t_len // TQ
    kv3 = winkv16.reshape(-1, LANE, KV_ROW)
    return pl.pallas_call(
        _win_body,
        grid=(n_batch, nq),
        in_specs=[pl.BlockSpec((TQ, NSA_W), lambda b, i: (b * nq + i, 0)),
                  pl.BlockSpec((nq, LANE, KV_ROW), lambda b, i: (b, 0, 0)),
                  pl.BlockSpec(toep.shape, lambda b, i: (0, 0, 0, 0))],
        out_specs=pl.BlockSpec((TQ, NSA_W), lambda b, i: (b * nq + i, 0)),
        out_shape=jax.ShapeDtypeStruct((n_batch * t_len, NSA_W), F32),
        compiler_params=_cparams("parallel", "parallel"),
        name="win_attn",
    )(qb16, kv3, toep)


SB_W = SB_HEADS * HEAD_DIM


def _split3(x):
    hi = x.astype(BF16)
    r1 = x - hi.astype(F32)
    mid = r1.astype(BF16)
    lo = (r1 - mid.astype(F32)).astype(BF16)
    return jnp.concatenate([hi, mid, lo], axis=-1)


def _log_sigmoid_neg(z):
    return -(jnp.maximum(z, 0.0) + jnp.log1p(jnp.exp(-jnp.abs(z))))


def _sb_body(q_ref, kv_ref, o_ref):
    qb = pl.program_id(1)
    tq = q_ref.shape[0]
    tpos = qb * tq + lax.broadcasted_iota(I32, (tq, LANE), 0)
    lane = lax.broadcasted_iota(I32, (tq, LANE), 1)
    jj = lax.broadcasted_iota(I32, (3 * LANE, LANE), 0) % LANE
    ss = lax.broadcasted_iota(I32, (3 * LANE, LANE), 1)
    later_mat = jnp.where(jj > ss, 1.0, 0.0).astype(BF16)
    for h in range(SB_HEADS):
        q = q_ref[:, h * HEAD_DIM:(h + 1) * HEAD_DIM]

        def step(i, carry, q=q, h=h):
            run, acc = carry
            kb = qb - i
            kv = kv_ref[kb]
            k = kv[:, h * HEAD_DIM:(h + 1) * HEAD_DIM]
            v = kv[:, SB_W + h * HEAD_DIM:SB_W + (h + 1) * HEAD_DIM]
            ok = kb * LANE + lane < tpos
            z = lax.dot_general(q, k, NT_DIMS, preferred_element_type=F32)
            lsn = _log_sigmoid_neg(z)
            log_1m = jnp.where(ok, lsn, 0.0)
            later = jnp.dot(_split3(log_1m), later_mat, preferred_element_type=F32) + run
            a = jnp.where(ok, jnp.exp(z + lsn + later), 0.0)
            acc = acc + jnp.dot(a.astype(BF16), v, preferred_element_type=F32)
            return run + jnp.sum(log_1m, axis=-1, keepdims=True), acc

        _, acc = lax.fori_loop(0, qb + 1, step, (jnp.zeros((tq, 1), F32), jnp.zeros((tq, HEAD_DIM), F32)))
        o_ref[:, h * HEAD_DIM:(h + 1) * HEAD_DIM] = acc


def _sb_attn(bq16, bkv16, n_batch, t_len):
    nq = t_len // TQ
    kv3 = bkv16.reshape(-1, LANE, 2 * SB_W)
    return pl.pallas_call(
        _sb_body,
        grid=(n_batch, nq),
        in_specs=[pl.BlockSpec((TQ, SB_W), lambda b, i: (b * nq + i, 0)),
                  pl.BlockSpec((nq, LANE, 2 * SB_W), lambda b, i: (b, 0, 0))],
        out_specs=pl.BlockSpec((TQ, SB_W), lambda b, i: (b * nq + i, 0)),
        out_shape=jax.ShapeDtypeStruct((n_batch * t_len, SB_W), F32),
        compiler_params=_cparams("parallel", "parallel"),
        name="sb_attn",
    )(bq16, kv3)


def _softmax_with_new(s, s_new):
    m = jnp.maximum(jnp.max(s, axis=-1, keepdims=True), s_new)
    e = jnp.exp(s - m)
    e_new = jnp.exp(s_new - m)
    return e, e_new, jnp.sum(e, axis=-1, keepdims=True) + e_new


def _decode_body(pt_ref, q_ref, bq_ref, selnew_ref, winnew_ref, *rest, n_pages):
    del pt_ref
    cmp_pages = rest[:n_pages]
    sel_pages = rest[n_pages:2 * n_pages]
    sb_pages = rest[2 * n_pages:3 * n_pages]
    win_ref, pe_ref, w1a_ref, w1b_ref, w2_ref, ov_ref, ex_ref, dsel_ref, dmisc_ref, o_ref = rest[3 * n_pages:]
    past = n_pages * LANE
    n_cmp = (past + 1 - CMP_BLOCK) // CMP_STRIDE + 1
    n_sb = -(-(past + 1) // SEL_BLOCK)
    lane = lax.broadcasted_iota(I32, (NSA_HEADS, LANE), 1)

    q = q_ref[0].astype(F32)
    zero = jnp.zeros((1, HEAD_DIM), F32)
    rows = []
    for h in range(NSA_HEADS):
        piece = q[:, h * HEAD_DIM:(h + 1) * HEAD_DIM]
        rows.append(jnp.concatenate([piece, zero] if h // NSA_GROUP == 0 else [zero, piece], axis=1))
    qblk = jnp.concatenate(rows, axis=0)
    qblk16 = qblk.astype(BF16)

    def own_head(of):
        return jnp.concatenate(
            [of[h:h + 1, (h // NSA_GROUP) * HEAD_DIM:(h // NSA_GROUP + 1) * HEAD_DIM] for h in range(NSA_HEADS)], axis=1)

    def new_key_score(knew):
        return jnp.sum(qblk * knew.astype(BF16).astype(F32), axis=-1, keepdims=True) + dmisc_ref[:, LANE:LANE + 1]

    x = jnp.concatenate([r[0] for r in cmp_pages], axis=0)
    kvc = _compress_math(x, pe_ref, w1a_ref, w1b_ref, w2_ref).astype(BF16)
    vis = lane < n_cmp
    s = lax.dot_general(qblk16, kvc[:, :KVW], NT_DIMS, preferred_element_type=F32) + dmisc_ref[:, :LANE]
    s = jnp.where(vis, s, NEG)
    e = jnp.exp(s - jnp.max(s, axis=-1, keepdims=True))
    p = jnp.where(vis, e / jnp.sum(e, axis=-1, keepdims=True), 0.0)
    o_cmp = own_head(jnp.dot(p.astype(BF16), kvc[:, KVW:], preferred_element_type=F32))

    pg = jnp.concatenate([jnp.sum(p[g * NSA_GROUP:(g + 1) * NSA_GROUP], axis=0, keepdims=True)
                          for g in range(NSA_KV_HEADS)], axis=0)
    score = jnp.dot(pg, ov_ref[...], preferred_element_type=F32, precision=lax.Precision.HIGHEST)
    sel = _select_mask(score, jnp.full(score.shape, past, I32), n_sb)
    sel_h = jnp.concatenate([sel[h // NSA_GROUP:h // NSA_GROUP + 1] for h in range(NSA_HEADS)], axis=0)
    chosen = jnp.dot(sel_h.astype(BF16), ex_ref[...], preferred_element_type=F32) > 0.5

    s = jnp.concatenate([lax.dot_general(qblk16, r[0][:, :KVW].astype(BF16), NT_DIMS, preferred_element_type=F32)
                         for r in sel_pages], axis=1)
    s = jnp.where(chosen, s + dsel_ref[...], NEG)
    new = selnew_ref[0]
    e, e_new, l = _softmax_with_new(s, new_key_score(new[:, :KVW]))
    of = e_new * new[:, KVW:].astype(BF16).astype(F32)
    for pg_i, r in enumerate(sel_pages):
        of = of + jnp.dot(e[:, pg_i * LANE:(pg_i + 1) * LANE].astype(BF16), r[0][:, KVW:].astype(BF16),
                          preferred_element_type=F32)
    o_sel = own_head(of / l)

    kw = win_ref[0]
    n_win = kw.shape[0]
    wpos = lax.broadcasted_iota(I32, (NSA_HEADS, n_win), 1)
    s = lax.dot_general(qblk16, kw[:, :KVW].astype(BF16), NT_DIMS, preferred_element_type=F32)
    s = jnp.where(n_win - wpos < WINDOW, s + dsel_ref[:, past - n_win:], NEG)
    new = winnew_ref[0]
    e, e_new, l = _softmax_with_new(s, new_key_score(new[:, :KVW]))
    of = e_new * new[:, KVW:].astype(BF16).astype(F32)
    of = of + jnp.dot(e.astype(BF16), kw[:, KVW:].astype(BF16), preferred_element_type=F32)
    o_win = own_head(of / l)

    l512 = lax.broadcasted_iota(I32, (SB_HEADS, SB_W), 1)
    r512 = lax.broadcasted_iota(I32, (SB_HEADS, SB_W), 0)
    bqblk = jnp.where(l512 // HEAD_DIM == r512, jnp.broadcast_to(bq_ref[0].astype(F32), (SB_HEADS, SB_W)), 0.0)
    bqblk16 = bqblk.astype(BF16)
    zs = [lax.dot_general(bqblk16, r[0][:, :SB_W].astype(BF16), NT_DIMS, preferred_element_type=F32)
          for r in sb_pages]
    lsn = [_log_sigmoid_neg(z) for z in zs]
    stacked = jnp.concatenate(lsn, axis=0)
    jj = lax.broadcasted_iota(I32, (3 * LANE, LANE), 0) % LANE
    ss = lax.broadcasted_iota(I32, (3 * LANE, LANE), 1)
    later_mat = jnp.where(jj > ss, 1.0, 0.0).astype(BF16)
    within = jnp.dot(_split3(stacked), later_mat, preferred_element_type=F32)
    total = jnp.sum(stacked, axis=-1, keepdims=True)
    run = jnp.zeros((SB_HEADS, 1), F32)
    of = jnp.zeros((SB_HEADS, SB_W), F32)
    for pg_i in reversed(range(n_pages)):
        rs = slice(pg_i * SB_HEADS, (pg_i + 1) * SB_HEADS)
        a = jnp.exp(zs[pg_i] + lsn[pg_i] + within[rs] + run)
        of = of + jnp.dot(a.astype(BF16), sb_pages[pg_i][0][:, SB_W:].astype(BF16), preferred_element_type=F32)
        run = run + total[rs]
    o_sb = jnp.concatenate([of[h:h + 1, h * HEAD_DIM:(h + 1) * HEAD_DIM] for h in range(SB_HEADS)], axis=1)

    o_ref[0] = jnp.concatenate([o_cmp, o_sel, o_win, o_sb], axis=1)


def _decode_attn(page_table, q16, bq16, selnew, winnew, cmp_pool, sel_pool, sb_pool, win_cache, cw, dsel, dmisc):
    n_dec, n_pages = page_table.shape
    past = n_pages * LANE
    w1a, w1b, pe, w2 = cw
    n_cmp = (past + 1 - CMP_BLOCK) // CMP_STRIDE + 1
    n_sb = -(-(past + 1) // SEL_BLOCK)
    ov = _overlap_matrix(n_cmp, n_sb)
    ex = jnp.asarray((np.arange(past)[None, :] // SEL_BLOCK == np.arange(LANE)[:, None]).astype(np.float32), dtype=BF16)
    n_pool = cmp_pool.shape[0]
    cmp3 = cmp_pool.reshape(n_pool, LANE // GROUP_ROWS, GROUP_ROWS * KV_ROW)
    sel3 = sel_pool.reshape(n_pool, LANE, KV_ROW)
    sb3 = sb_pool.reshape(n_pool, LANE, 2 * SB_W)
    win3 = win_cache.reshape(n_dec, -1, KV_ROW)

    def page_spec(arr, p):
        return pl.BlockSpec((1,) + arr.shape[1:], lambda b, pt, p=p: (pt[b, p], 0, 0))

    row = lambda arr: pl.BlockSpec((1,) + arr.shape[1:], lambda b, pt: (b, 0, 0))
    const = lambda arr: pl.BlockSpec(arr.shape, lambda b, pt: (0,) * arr.ndim)
    once = lambda arr: pl.BlockSpec(arr.shape, lambda b, pt: (0,) * arr.ndim, pipeline_mode=pl.Buffered(1))
    in_specs = ([row(q16), row(bq16), row(selnew), row(winnew)]
                + [page_spec(cmp3, p) for p in range(n_pages)]
                + [page_spec(sel3, p) for p in range(n_pages)]
                + [page_spec(sb3, p) for p in range(n_pages)]
                + [row(win3), const(pe), once(w1a), once(w1b), const(w2), const(ov), const(ex), const(dsel), const(dmisc)])
    out_w = 3 * NSA_W + SB_W
    return pl.pallas_call(
        functools.partial(_decode_body, n_pages=n_pages),
        grid_spec=pltpu.PrefetchScalarGridSpec(
            num_scalar_prefetch=1,
            grid=(n_dec,),
            in_specs=in_specs,
            out_specs=pl.BlockSpec((1, 1, out_w), lambda b, pt: (b, 0, 0))),
        out_shape=jax.ShapeDtypeStruct((n_dec, 1, out_w), F32),
        compiler_params=_cparams("arbitrary"),
        name="decode_attn",
    )(page_table, q16, bq16, selnew, winnew, *([cmp3] * n_pages), *([sel3] * n_pages), *([sb3] * n_pages),
      win3, pe, w1a, w1b, w2, ov, ex, dsel, dmisc)


_COL = dict(q=(0, 512), cmp=(512, 768), sel=(768, 1024), win=(1024, 1280), bq=(1280, 1792), bkv=(1792, 2816),
            gl=(2816, 2944))


def _permute_w_in(w_in):
    offs = np.concatenate([[0], np.cumsum([512, 128, 128, 128, 128, 128, 128, 24, 512, 512, 512])])
    seg = lambda i: w_in[:, offs[i]:offs[i + 1]]
    pad = jnp.zeros((w_in.shape[0], LANE - N_BRANCH * NSA_HEADS), w_in.dtype)
    return jnp.concatenate([seg(0), seg(1), seg(2), seg(3), seg(4), seg(5), seg(6), seg(8), seg(9), seg(10), seg(7), pad],
                           axis=1).astype(BF16)


_ATTN_OUTS = (_COL['q'] + (BF16, SCALE), _COL['bq'] + (BF16, SCALE), _COL['gl'] + (F32, 1.0),
              _COL['cmp'] + (F32, 1.0), _COL['sel'] + (F32, 1.0), _COL['win'] + (F32, 1.0), _COL['bkv'] + (F32, 1.0),
              _COL['sel'] + (BF16, 1.0), _COL['win'] + (BF16, 1.0), _COL['bkv'] + (BF16, 1.0))


def _attn_layer(y, gamma, w_in_perm, w_out_bf16, cw, tables, caches, page_table, n_batch, t_len):
    cb, toep, dsel, dmisc = tables
    cache_cmp, cache_sel, cache_win, cache_sb = caches
    ntp = y.shape[0]
    n_prompt = n_batch * t_len
    n_dec = page_table.shape[0]
    q16, bq16, gl, cmpkv, selkv, winkv, bkv, selkv16, winkv16, bkv16 = _rms_proj(y, gamma, w_in_perm, _ATTN_OUTS)

    kvc = _compress(cmpkv.reshape(ntp // GROUP_ROWS, GROUP_ROWS * KV_ROW), n_batch, cw)
    o_cmp, selm = _cmp_select(q16, kvc, cb, n_batch, t_len)
    o_sel = _sel_attn(q16, selkv16, selm, toep, n_batch, t_len)
    o_win = _win_attn(q16, winkv16, toep, n_batch, t_len)
    o_sb = _sb_attn(bq16, bkv16, n_batch, t_len)

    dec = slice(n_prompt, n_prompt + n_dec)
    od = _decode_attn(page_table, q16[dec][:, None], bq16[dec][:, None], selkv[dec][:, None], winkv[dec][:, None],
                      cache_cmp, cache_sel, cache_sb, cache_win, cw, dsel[:, 0], dmisc[:, 0])[:, 0]
    pad = jnp.zeros((ntp - n_prompt - n_dec, NSA_W), F32)
    full = lambda prompt_part, j: jnp.concatenate([prompt_part, od[:, j * NSA_W:(j + 1) * NSA_W], pad], axis=0)
    y = _attn_out(y, full(o_cmp, 0), full(o_sel, 1), full(o_win, 2), full(o_sb, 3), gl, w_out_bf16)

    kvshape = lambda n, a, heads: a.reshape(n, -1, 2, heads, HEAD_DIM)
    keep = min(WINDOW, t_len)
    p_state = (kvshape(n_batch, cmpkv[:n_prompt], NSA_KV_HEADS), kvshape(n_batch, selkv[:n_prompt], NSA_KV_HEADS),
               kvshape(n_batch, winkv[:n_prompt], NSA_KV_HEADS)[:, t_len - keep:], kvshape(n_batch, bkv[:n_prompt], SB_HEADS))
    new_win = kvshape(n_dec, winkv[dec], NSA_KV_HEADS)
    win_all = jnp.concatenate([cache_win, new_win], axis=1)
    s_state = (kvshape(n_dec, cmpkv[dec], NSA_KV_HEADS), kvshape(n_dec, selkv[dec], NSA_KV_HEADS),
               win_all[:, win_all.shape[1] - min(WINDOW, win_all.shape[1]):], kvshape(n_dec, bkv[dec], SB_HEADS))
    return y, p_state, s_state


def kernel(x_prompt, x_sample, cache_cmp_kv, cache_sel_kv, cache_win_kv, cache_sb_kv, page_table, rel_bias, mix_norm, ffn_norm, final_norm, a_w_in, a_w_out, cmp_pe, cmp_w1, cmp_w2, c_w_in, c_w_s, c_b_s, c_ln_g, c_ln_b, c_w_out, peer_w_q, peer_keys, peer_u, peer_v):
    n_batch, t_len, d = x_prompt.shape
    n_dec = x_sample.shape[0]
    assert x_sample.shape[1] == 1 and t_len % ROW_TILE == 0 and t_len == page_table.shape[1] * cache_cmp_kv.shape[2]
    n_prompt = n_batch * t_len
    ntp = -(-(n_prompt + n_dec) // ROW_TILE) * ROW_TILE
    y = jnp.concatenate([x_prompt.reshape(n_prompt, d), x_sample.reshape(n_dec, d),
                         jnp.zeros((ntp - n_prompt - n_dec, d), F32)], axis=0)
    depth = mix_norm.shape[0]
    tables = _bias_tables(rel_bias, t_len, t_len)
    p_states, s_states, s_chunk = [], [], []
    for layer in range(depth):
        i = layer // 2
        if layer % 2 == 0:
            cw = _compress_weights(cmp_pe[i], cmp_w1[i], cmp_w2[i])
            caches = (cache_cmp_kv[i], cache_sel_kv[i], cache_win_kv[i], cache_sb_kv[i])
            y, p_st, s_st = _attn_layer(y, mix_norm[layer], _permute_w_in(a_w_in[i]), a_w_out[i].astype(BF16), cw, tables,
                                        caches, page_table, n_batch, t_len)
            p_states.append(p_st)
            s_states.append(s_st)
        else:
            bs_t = jnp.transpose(c_b_s[i])
            gd = c_ln_g.shape[1] // C_GROUPS
            bs_map = jnp.stack([jnp.repeat(bs_t, gd, axis=1),
                                jnp.repeat(jnp.broadcast_to(bs_t[0:1], bs_t.shape), gd, axis=1)])
            y, v = _chunk_layer(y, mix_norm[layer], c_w_in[i].astype(BF16), c_w_s[i], bs_map, c_ln_g[i], c_ln_b[i],
                                c_w_out[i].astype(BF16), n_prompt)
            s_chunk.append(v[n_prompt:n_prompt + n_dec].reshape(n_dec, 1, -1))
        y = _peer_layer(y, ffn_norm[layer], peer_w_q[layer].astype(BF16),
                        peer_keys[layer].reshape(2 * PEER_HEADS, N_KEYS, -1).astype(BF16),
                        peer_u[layer].astype(BF16), peer_v[layer].astype(BF16))
    y = _final_norm(y, final_norm)
    y_prompt = y[:n_prompt].reshape(n_batch, t_len, d)
    y_sample = y[n_prompt:n_prompt + n_dec].reshape(n_dec, 1, d)
    stack = lambda states, j: jnp.stack([st[j] for st in states])
    return (y_prompt, y_sample, stack(p_states, 0), stack(p_states, 1), stack(p_states, 2), stack(p_states, 3),
            stack(s_states, 0), stack(s_states, 1), stack(s_states, 2), stack(s_states, 3), jnp.stack(s_chunk))
```
